```python
import math
import jax, jax.numpy as jnp
from jax import lax
import numpy as np

D_MODEL = 1024
BATCH = 8
SEQ = 4096
DEPTH = 1

MIX_WIDTH = D_MODEL
DIFF_WIDTH = MIX_WIDTH // 2
RWKV_WIDTH = MIX_WIDTH - DIFF_WIDTH
DIFF_HEAD_DIM = 64
DIFF_HEADS = DIFF_WIDTH // (2 * DIFF_HEAD_DIM)
RWKV_HEAD_SIZE = 64
RWKV_HEADS = RWKV_WIDTH // RWKV_HEAD_SIZE
DECAY_LORA = 64
ICLR_LORA = 64
GATE_LORA = 128
IN_WIDTH = 3 * DIFF_WIDTH + 3 * RWKV_WIDTH
FFN_HIDDEN = ((8 * D_MODEL + 3 * 256 - 1) // (3 * 256)) * 256
Q_BLOCK = 128
RMS_EPS = 1e-6
SUBLN_EPS = 1e-5
LNX_EPS = 1e-5 * RWKV_HEAD_SIZE

kernel_name = "hymba_diffattn_rwkv7_sandwich_adaln"


def rms_norm(x, g, eps=RMS_EPS):
    xf = x.astype(jnp.float32)
    y = xf * lax.rsqrt(jnp.mean(xf * xf, axis=-1, keepdims=True) + eps)
    return (y * g.astype(jnp.float32)).astype(x.dtype)


def token_shift(t):
    return jnp.pad(t, ((0, 0), (1, 0), (0, 0)))[:, :-1]


def diff_attention(q, k, v, lam, lam_init, subln_g):
    B, S = q.shape[0], q.shape[1]
    scale = DIFF_HEAD_DIM ** -0.5
    vf = v.astype(jnp.float32)
    outs = []
    for i in range(S // Q_BLOCK):
        start, end = i * Q_BLOCK, (i + 1) * Q_BLOCK
        qb = q[:, start:end]
        kb = k[:, :end]
        s = jnp.einsum('bqhcd,bkhcd->bhcqk', qb, kb).astype(jnp.float32) * scale
        qpos = start + jnp.arange(Q_BLOCK)
        kpos = jnp.arange(end)
        mask = kpos[None, :] <= qpos[:, None]
        p = jax.nn.softmax(jnp.where(mask, s, -jnp.inf), axis=-1)
        a = p[:, :, 0] - lam * p[:, :, 1]
        outs.append(jnp.einsum('bhqk,bkhe->bqhe', a, vf[:, :end]))
    o = jnp.concatenate(outs, axis=1)
    o = rms_norm(o, subln_g, SUBLN_EPS) * (1.0 - lam_init)
    return o.reshape(B, S, DIFF_WIDTH)


def wkv7_scan(r, w, k, v, a, b):
    B, _, H, N = r.shape

    def step(state, inp):
        r_t, w_t, k_t, v_t, a_t, b_t = inp
        sa = jnp.einsum('bhvk,bhk->bhv', state, a_t)
        state = (state * w_t[:, :, None, :] + sa[..., None] * b_t[:, :, None, :]
                 + v_t[..., None] * k_t[:, :, None, :])
        return state, jnp.einsum('bhvk,bhk->bhv', state, r_t)

    xs = tuple(jnp.moveaxis(t, 1, 0) for t in (r, w, k, v, a, b))
    state0 = jnp.zeros((B, H, N, N), jnp.float32)
    _, y = lax.scan(step, state0, xs)
    return jnp.moveaxis(y, 0, 1)


def rwkv7_mix(h, p_r, p_k, p_v, mu_r, mu_k, mu_v, mu_w, mu_a, mu_g,
              w0, w1, w2, a0, a1, a2, g1, g2, k_k, k_a, r_k, lnx_g, lnx_b):
    B, S, _ = h.shape
    H, N = RWKV_HEADS, RWKV_HEAD_SIZE
    f32 = jnp.float32
    r = p_r + (token_shift(p_r) - p_r) * mu_r
    k = p_k + (token_shift(p_k) - p_k) * mu_k
    v = p_v + (token_shift(p_v) - p_v) * mu_v
    dh = token_shift(h) - h
    xw = h + dh * mu_w
    xa = h + dh * mu_a
    xg = h + dh * mu_g
    w_log = -jax.nn.softplus(-(w0 + jnp.tanh(xw @ w1) @ w2).astype(f32)) - 0.5
    decay = jnp.exp(-jnp.exp(w_log))
    iclr = jax.nn.sigmoid((a0 + (xa @ a1) @ a2).astype(f32))
    g = (jax.nn.sigmoid(xg @ g1) @ g2).astype(f32)
    kf = k.astype(f32)
    kk = (kf * k_k.astype(f32)).reshape(B, S, H, N)
    kk = kk / jnp.maximum(jnp.sqrt(jnp.sum(kk * kk, axis=-1, keepdims=True)), 1e-12)
    kf = kf * (1.0 + (iclr - 1.0) * k_a.astype(f32))
    heads = lambda t: t.reshape(B, S, H, N)
    r_h, k_h, v_h = heads(r.astype(f32)), heads(kf), heads(v.astype(f32))
    a_h = heads(iclr)
    y = wkv7_scan(r_h, heads(decay), k_h, v_h, -kk, kk * a_h)
    mean = jnp.mean(y, axis=-1, keepdims=True)
    var = jnp.mean(jnp.square(y - mean), axis=-1, keepdims=True)
    y = (y - mean) * lax.rsqrt(var + LNX_EPS)
    y = y * lnx_g.astype(f32).reshape(H, N) + lnx_b.astype(f32).reshape(H, N)
    bonus = jnp.sum(r_h * k_h * r_k.astype(f32), axis=-1, keepdims=True) * v_h
    return (y + bonus).reshape(B, S, RWKV_WIDTH) * g


def setup_inputs(seed: int = 0) -> dict:
    key = jax.random.key(seed)
    ks = iter(jax.random.split(key, 48))
    nrm = lambda shape, s: jax.random.normal(next(ks), shape, jnp.float32) * s
    uni = lambda shape, lo, hi: jax.random.uniform(next(ks), shape, jnp.float32, lo, hi)
    L, D = DEPTH, D_MODEL
    gain = lambda n: 1.0 + nrm((L, n), 0.02)
    return {
        "x": nrm((BATCH, SEQ, D), 1.0),
        "c": nrm((BATCH, D), 1.0),
        "w_ada": nrm((L, D, 6 * D), 0.5 * D ** -0.5),
        "b_ada": nrm((L, 6 * D), 0.02),
        "g_pre_mix": gain(D),
        "g_post_mix": gain(D),
        "g_pre_ffn": gain(D),
        "g_post_ffn": gain(D),
        "w_in": nrm((L, D, IN_WIDTH), D ** -0.5),
        "lam_q1": nrm((L, DIFF_HEAD_DIM), 0.1),
        "lam_k1": nrm((L, DIFF_HEAD_DIM), 0.1),
        "lam_q2": nrm((L, DIFF_HEAD_DIM), 0.1),
        "lam_k2": nrm((L, DIFF_HEAD_DIM), 0.1),
        "subln_g": gain(2 * DIFF_HEAD_DIM),
        "mu_r": uni((L, RWKV_WIDTH), 0.0, 1.0),
        "mu_k": uni((L, RWKV_WIDTH), 0.0, 1.0),
        "mu_v": uni((L, RWKV_WIDTH), 0.0, 1.0),
        "mu_w": uni((L, D), 0.0, 1.0),
        "mu_a": uni((L, D), 0.0, 1.0),
        "mu_g": uni((L, D), 0.0, 1.0),
        "w0": uni((L, RWKV_WIDTH), -6.0, 1.0),
        "w1": nrm((L, D, DECAY_LORA), D ** -0.5),
        "w2": nrm((L, DECAY_LORA, RWKV_WIDTH), 0.5 * DECAY_LORA ** -0.5),
        "a0": nrm((L, RWKV_WIDTH), 0.1),
        "a1": nrm((L, D, ICLR_LORA), D ** -0.5),
        "a2": nrm((L, ICLR_LORA, RWKV_WIDTH), 0.5 * ICLR_LORA ** -0.5),
        "g1": nrm((L, D, GATE_LORA), D ** -0.5),
        "g2": nrm((L, GATE_LORA, RWKV_WIDTH), GATE_LORA ** -0.5),
        "k_k": 0.85 + nrm((L, RWKV_WIDTH), 0.02),
        "k_a": 1.0 + nrm((L, RWKV_WIDTH), 0.02),
        "r_k": nrm((L, RWKV_HEADS, RWKV_HEAD_SIZE), 0.1),
        "lnx_g": gain(RWKV_WIDTH),
        "lnx_b": nrm((L, RWKV_WIDTH), 0.02),
        "w_out": nrm((L, MIX_WIDTH, D), MIX_WIDTH ** -0.5),
        "w_gate": nrm((L, D, FFN_HIDDEN), D ** -0.5),
        "w_up": nrm((L, D, FFN_HIDDEN), D ** -0.5),
        "w_down": nrm((L, FFN_HIDDEN, D), FFN_HIDDEN ** -0.5),
    }


def reference(x, c, w_ada, b_ada, g_pre_mix, g_post_mix, g_pre_ffn, g_post_ffn, w_in,
              lam_q1, lam_k1, lam_q2, lam_k2, subln_g, mu_r, mu_k, mu_v, mu_w, mu_a, mu_g,
              w0, w1, w2, a0, a1, a2, g1, g2, k_k, k_a, r_k, lnx_g, lnx_b, w_out,
              w_gate, w_up, w_down):
    B, S, D = x.shape
    split_at = [DIFF_WIDTH, 2 * DIFF_WIDTH, 3 * DIFF_WIDTH,
                3 * DIFF_WIDTH + RWKV_WIDTH, 3 * DIFF_WIDTH + 2 * RWKV_WIDTH]
    for l in range(DEPTH):
        ada = jax.nn.silu(c) @ w_ada[l] + b_ada[l]
        sh_m, sc_m, gt_m, sh_f, sc_f, gt_f = jnp.split(ada, 6, axis=-1)

        h = rms_norm(x, g_pre_mix[l]) * (1.0 + sc_m[:, None]) + sh_m[:, None]
        proj = h @ w_in[l]
        dq, dk, dv, pr, pk, pv = jnp.split(proj, split_at, axis=-1)

        lam_init = 0.8 - 0.6 * math.exp(-0.3 * l)
        lam = (jnp.exp(jnp.sum(lam_q1[l].astype(jnp.float32) * lam_k1[l].astype(jnp.float32)))
               - jnp.exp(jnp.sum(lam_q2[l].astype(jnp.float32) * lam_k2[l].astype(jnp.float32)))
               + lam_init)
        y_diff = diff_attention(
            dq.reshape(B, S, DIFF_HEADS, 2, DIFF_HEAD_DIM),
            dk.reshape(B, S, DIFF_HEADS, 2, DIFF_HEAD_DIM),
            dv.reshape(B, S, DIFF_HEADS, 2 * DIFF_HEAD_DIM),
            lam, lam_init, subln_g[l])

        y_rwkv = rwkv7_mix(h, pr, pk, pv, mu_r[l], mu_k[l], mu_v[l], mu_w[l], mu_a[l], mu_g[l],
                           w0[l], w1[l], w2[l], a0[l], a1[l], a2[l], g1[l], g2[l],
                           k_k[l], k_a[l], r_k[l], lnx_g[l], lnx_b[l])

        y_mix = jnp.concatenate([y_diff, y_rwkv], axis=-1).astype(x.dtype) @ w_out[l]
        x = x + gt_m[:, None] * rms_norm(y_mix, g_post_mix[l])

        h = rms_norm(x, g_pre_ffn[l]) * (1.0 + sc_f[:, None]) + sh_f[:, None]
        f = (jax.nn.silu(h @ w_gate[l]) * (h @ w_up[l])) @ w_down[l]
        x = x + gt_f[:, None] * rms_norm(f, g_post_ffn[l])
    return x
```

```python
import functools
import math

import jax
import jax.numpy as jnp
from jax import lax
from jax.experimental import pallas as pl
from jax.experimental.pallas import tpu as pltpu

F32 = jnp.float32
BF16 = jnp.bfloat16

D_MODEL = 1024
DIFF_WIDTH = 512
RWKV_WIDTH = 512
DIFF_HEAD_DIM = 64
DIFF_HEADS = 4
HEAD = 64
PAIR = 2 * HEAD
N_PAIRS = RWKV_WIDTH // PAIR
CHUNK = 64
LORA_PAD = 256
IN_WIDTH = 3 * DIFF_WIDTH + 3 * RWKV_WIDTH
FFN_HIDDEN = 2816
RMS_EPS = 1e-6
SUBLN_EPS = 1e-5
LNX_EPS = 1e-5 * HEAD
EXP_M05 = math.exp(-0.5)
NEG_BIG = -1e30
VMEM_LIMIT = 56 * 1024 * 1024

NN = (((1,), (0,)), ((), ()))
NT = (((1,), (1,)), ((), ()))


def _dot(a, b, dims=NN):
    return lax.dot_general(a, b, dims, preferred_element_type=F32)


def _split2(x):
    hi = x.astype(BF16)
    lo = (x - hi.astype(F32)).astype(BF16)
    return hi, lo


def _mm(a, b, dims=NN, passes=1):
    if passes == 1:
        return _dot(a.astype(BF16), b.astype(BF16), dims)
    ah, al = _split2(a)
    bh, bl = _split2(b)
    return _dot(ah, bh, dims) + (_dot(ah, bl, dims) + _dot(al, bh, dims))


def _rms(x, eps):
    return x * lax.rsqrt(jnp.mean(x * x, axis=-1, keepdims=True) + eps)


def _sigmoid(x):
    return 1.0 / (1.0 + jnp.exp(-x))


def _ada_kernel(c_ref, w_ref, b_ref, o_ref):
    c = c_ref[...]
    s = c * _sigmoid(c)
    o_ref[...] = _mm(s, w_ref[...], passes=3) + b_ref[...]


def _ada(c, w_ada, b_ada):
    B, D = c.shape
    n_out = w_ada.shape[1]
    tn = 1024
    return pl.pallas_call(
        _ada_kernel,
        grid=(n_out // tn,),
        in_specs=[pl.BlockSpec((B, D), lambda j: (0, 0)),
                  pl.BlockSpec((D, tn), lambda j: (0, j)),
                  pl.BlockSpec((1, tn), lambda j: (0, j))],
        out_specs=pl.BlockSpec((B, tn), lambda j: (0, j)),
        out_shape=jax.ShapeDtypeStruct((B, n_out), F32),
        compiler_params=pltpu.CompilerParams(dimension_semantics=("arbitrary",),
                                             vmem_limit_bytes=VMEM_LIMIT),
        name="ada",
    )(c, w_ada, b_ada.reshape(1, n_out))


def _shift_rows(t, first_row):
    row = lax.broadcasted_iota(jnp.int32, t.shape, 0)
    return jnp.where(row == 0, first_row, pltpu.roll(t, 1, 0))


def _premix_kernel(x_ref, ada_ref, gpre_ref, win_ref, wl1_ref, muh_ref, murkv_ref,
                   q_ref, k_ref, v_ref, rr_ref, rk_ref, rv_ref, l1_ref,
                   hcarry, pcarry):
    tm = x_ref.shape[1]

    @pl.when(pl.program_id(1) == 0)
    def _():
        hcarry[...] = jnp.zeros_like(hcarry)
        pcarry[...] = jnp.zeros_like(pcarry)

    x = x_ref[0]
    ada = ada_ref[0]
    sh, sc = ada[0:1], ada[1:2]
    h = _rms(x, RMS_EPS) * gpre_ref[...]
    h = h * (1.0 + sc) + sh

    hs = _shift_rows(h, hcarry[0:1, :])
    hcarry[0:1, :] = h[tm - 1:tm, :]
    dh = hs - h
    z = (_dot((h + dh * muh_ref[0:1, :]).astype(BF16), wl1_ref[0])
         + _dot((h + dh * muh_ref[1:2, :]).astype(BF16), wl1_ref[1])
         + _dot((h + dh * muh_ref[2:3, :]).astype(BF16), wl1_ref[2]))
    lane = lax.broadcasted_iota(jnp.int32, z.shape, 1)
    l1 = jnp.where(lane < 64, jnp.tanh(z), jnp.where(lane < 128, z, _sigmoid(z)))
    l1_ref[0] = l1.astype(BF16)

    hb = h.astype(BF16)
    scale = DIFF_HEAD_DIM ** -0.5
    q_ref[0] = (_dot(hb, win_ref[:, 0:512]) * scale).astype(BF16)
    k_ref[0] = _dot(hb, win_ref[:, 512:1024]).astype(BF16)
    v_ref[0] = _dot(hb, win_ref[:, 1024:1536]).astype(BF16)
    for i, o_ref in enumerate((rr_ref, rk_ref, rv_ref)):
        p = _dot(hb, win_ref[:, 1536 + 512 * i:2048 + 512 * i])
        ps = _shift_rows(p, pcarry[i:i + 1, :])
        pcarry[i:i + 1, :] = p[tm - 1:tm, :]
        o_ref[0] = p + (ps - p) * murkv_ref[i:i + 1, :]


def _premix(x, ada3, g_pre, w_in, wl1, mu_h, mu_rkv, tm=512):
    B, S, D = x.shape
    tm = min(tm, S)
    tok = lambda w: pl.BlockSpec((1, tm, w), lambda b, s: (b, s, 0))
    const2 = lambda a: pl.BlockSpec(a.shape, lambda b, s: (0, 0))
    const3 = lambda a: pl.BlockSpec(a.shape, lambda b, s: (0, 0, 0))
    out_shapes = ([jax.ShapeDtypeStruct((B, S, DIFF_WIDTH), BF16)] * 3
                  + [jax.ShapeDtypeStruct((B, S, RWKV_WIDTH), F32)] * 3
                  + [jax.ShapeDtypeStruct((B, S, LORA_PAD), BF16)])
    return pl.pallas_call(
        _premix_kernel,
        grid=(B, S // tm),
        in_specs=[tok(D),
                  pl.BlockSpec((1, 6, D), lambda b, s: (b, 0, 0)),
                  const2(g_pre), const2(w_in), const3(wl1), const2(mu_h), const2(mu_rkv)],
        out_specs=[tok(512)] * 6 + [tok(LORA_PAD)],
        out_shape=out_shapes,
        scratch_shapes=[pltpu.VMEM((8, D), F32), pltpu.VMEM((8, RWKV_WIDTH), F32)],
        compiler_params=pltpu.CompilerParams(dimension_semantics=("parallel", "arbitrary"),
                                             vmem_limit_bytes=VMEM_LIMIT),
        name="premix",
    )(x, ada3, g_pre, w_in, wl1, mu_h, mu_rkv)


def _attn_kernel(lamv_ref, g_ref, q_ref, k_ref, v_ref, o_ref, *, lam_init):
    tq = q_ref.shape[1]
    tk = tq
    i = pl.program_id(2)
    q = q_ref[0]
    lane = lax.broadcasted_iota(jnp.int32, q.shape, 1)
    zero = jnp.zeros_like(q)
    qs = jnp.concatenate([jnp.where(lane < 64, q, zero), jnp.where(lane < 64, zero, q)], axis=0)

    def step(j, carry, masked):
        m, l, acc = carry
        kj = k_ref[0, pl.ds(pl.multiple_of(j * tk, tk), tk), :]
        vj = v_ref[0, pl.ds(pl.multiple_of(j * tk, tk), tk), :]
        s = _dot(qs, kj, NT)
        if masked:
            row = lax.broadcasted_iota(jnp.int32, (tq, tk), 0)
            col = lax.broadcasted_iota(jnp.int32, (tq, tk), 1)
            keep = jnp.concatenate([col <= row, col <= row], axis=0)
            s = jnp.where(keep, s, NEG_BIG)
        m_new = jnp.maximum(m, jnp.max(s, axis=-1, keepdims=True))
        alpha = jnp.exp(m - m_new)
        p = jnp.exp(s - m_new)
        l = alpha * l + jnp.sum(p, axis=-1, keepdims=True)
        acc = alpha * acc + _dot(p.astype(BF16), vj)
        return m_new, l, acc

    init = (jnp.full((2 * tq, 1), NEG_BIG, F32), jnp.zeros((2 * tq, 1), F32),
            jnp.zeros((2 * tq, 2 * DIFF_HEAD_DIM), F32))
    carry = lax.fori_loop(0, i, functools.partial(step, masked=False), init)
    m, l, acc = step(i, carry, True)

    lamv = lamv_ref[...]
    lam = (jnp.exp(jnp.sum(lamv[0:1] * lamv[1:2], axis=-1, keepdims=True))
           - jnp.exp(jnp.sum(lamv[2:3] * lamv[3:4], axis=-1, keepdims=True)) + lam_init)
    o = acc / l
    o = o[:tq] - lam * o[tq:]
    o = _rms(o, SUBLN_EPS) * g_ref[...] * (1.0 - lam_init)
    o_ref[0] = o.astype(BF16)


def _attn(q, k, v, lamv, subln_g, lam_init, tq=256):
    B, S, _ = q.shape
    return pl.pallas_call(
        functools.partial(_attn_kernel, lam_init=lam_init),
        grid=(B, DIFF_HEADS, S // tq),
        in_specs=[pl.BlockSpec(lamv.shape, lambda b, h, i: (0, 0)),
                  pl.BlockSpec(subln_g.shape, lambda b, h, i: (0, 0)),
                  pl.BlockSpec((1, tq, 128), lambda b, h, i: (b, i, h)),
                  pl.BlockSpec((1, S, 128), lambda b, h, i: (b, 0, h)),
                  pl.BlockSpec((1, S, 128), lambda b, h, i: (b, 0, h))],
        out_specs=pl.BlockSpec((1, tq, 128), lambda b, h, i: (b, i, h)),
        out_shape=jax.ShapeDtypeStruct((B, S, DIFF_WIDTH), BF16),
        compiler_params=pltpu.CompilerParams(
            dimension_semantics=("parallel", "parallel", "arbitrary"),
            vmem_limit_bytes=VMEM_LIMIT),
        name="diff_attn",
    )(lamv, subln_g, q, k, v)


P_PAIRWISE = 3
P_SOLVE = 3
P_LOCAL = 3
P_STATE = 3


def _wkv_kernel(r_ref, k_ref, v_ref, l1_ref, w2_ref, vec_ref, o_ref, st_ref, y_ref, *, n_chunks):
    @pl.when(pl.program_id(1) == 0)
    def _():
        st_ref[...] = jnp.zeros_like(st_ref)

    r = r_ref[0]
    k = k_ref[0]
    v = v_ref[0]
    l1 = l1_ref[0]
    vec = vec_ref[...]
    w0, a0, k_k, k_a, r_k, lnx_g, lnx_b = (vec[i:i + 1] for i in range(7))

    lw = -EXP_M05 * _sigmoid(w0 + _dot(l1, w2_ref[0]))
    iclr = _sigmoid(a0 + _dot(l1, w2_ref[1]))
    gate = _dot(l1, w2_ref[2])

    ri = lax.broadcasted_iota(jnp.int32, (PAIR, PAIR), 0)
    ci = lax.broadcasted_iota(jnp.int32, (PAIR, PAIR), 1)
    same_head = (ri // HEAD) == (ci // HEAD)
    head_ones = jnp.where(same_head, 1.0, 0.0).astype(BF16)

    def head_sum(t):
        tb = t.astype(BF16)
        return jnp.concatenate(
            [_dot(tb[:, PAIR * p:PAIR * (p + 1)], head_ones) for p in range(N_PAIRS)], axis=1)

    kk = k * k_k
    kk = kk / jnp.maximum(jnp.sqrt(head_sum(kk * kk)), 1e-12)
    kf = k * (1.0 + (iclr - 1.0) * k_a)
    b = kk * iclr

    tr = lax.broadcasted_iota(jnp.int32, (CHUNK, CHUNK), 0)
    tc = lax.broadcasted_iota(jnp.int32, (CHUNK, CHUNK), 1)
    tri = jnp.where(tr >= tc, 1.0, 0.0).astype(BF16)
    lw_hi = lw.astype(BF16)
    lw_r = lw - lw_hi.astype(F32)
    lw_mid = lw_r.astype(BF16)
    lw_lo = (lw_r - lw_mid.astype(F32)).astype(BF16)

    lane = lax.broadcasted_iota(jnp.int32, (CHUNK, PAIR), 1)
    first = lane < HEAD

    def stack(t):
        return jnp.concatenate([jnp.where(first, t, 0.0), jnp.where(first, 0.0, t)], axis=0)

    strict = ri > ci
    incl = ri >= ci
    diag = ri == ci

    for c in range(n_chunks):
        rows = slice(CHUNK * c, CHUNK * (c + 1))
        cw_all = (_dot(tri, lw_hi[rows]) + _dot(tri, lw_mid[rows])) + _dot(tri, lw_lo[rows])
        for p in range(N_PAIRS):
            lanes = slice(PAIR * p, PAIR * (p + 1))
            cw = cw_all[:, lanes]
            lwc = lw[rows, lanes]
            cw_end = cw[CHUNK - 1:CHUNK, :]
            w_in = jnp.exp(cw)
            w_ex = jnp.exp(cw - lwc)
            w_inv = jnp.exp(-cw)
            w_end = jnp.exp(cw_end)
            w_rest = jnp.exp(cw_end - cw)
            kkc, bc, kfc = kk[rows, lanes], b[rows, lanes], kf[rows, lanes]
            rc, vc = r[rows, lanes], v[rows, lanes]
            a_t = stack(-kkc * w_ex)
            r_t = rc * w_in
            b_t = stack(bc * w_inv)
            k_t = stack(kfc * w_inv)
            b_h = stack(bc * w_rest)
            k_h = stack(kfc * w_rest)
            v_s = stack(vc)

            pw = _mm(jnp.concatenate([a_t, stack(r_t)], axis=0),
                     jnp.concatenate([b_t, k_t], axis=0), NT, P_PAIRWISE)
            l_ab = jnp.where(strict, pw[:PAIR, :PAIR], 0.0)
            l_ak = jnp.where(strict, pw[:PAIR, PAIR:], 0.0)
            l_rb = jnp.where(incl, pw[PAIR:, :PAIR], 0.0)
            l_rk = jnp.where(incl, pw[PAIR:, PAIR:], 0.0)

            xs = jnp.concatenate([a_t, _mm(l_ak, v_s, NN, P_SOLVE)], axis=1)
            pwr = l_ab
            for it in range(6):
                xs = xs + _mm(pwr, xs, NN, P_SOLVE)
                if it < 5:
                    pwr = _mm(pwr, pwr, NN, P_SOLVE)

            rhs = jnp.concatenate(
                [xs, jnp.concatenate([jnp.zeros((PAIR, PAIR), F32), v_s], axis=1)], axis=0)
            ry = _mm(jnp.concatenate([l_rb, l_rk], axis=1), rhs, NN, P_LOCAL)
            r_hat = r_t + ry[:CHUNK, :PAIR] + ry[CHUNK:, :PAIR]
            y_loc = ry[:CHUNK, PAIR:] + ry[CHUNK:, PAIR:]
            bk_t = jnp.concatenate([b_h, k_h], axis=0).T
            mn = _mm(bk_t, rhs, NN, P_LOCAL)
            m_t = mn[:, :PAIR] + jnp.where(diag, w_end, 0.0)
            n_t = mn[:, PAIR:]

            st = st_ref[p]
            y_ref[rows, lanes] = _mm(r_hat, st, NN, P_STATE) + y_loc
            st_ref[p] = _mm(m_t, st, NN, P_STATE) + n_t

    y = y_ref[...]
    mean = head_sum(y) * (1.0 / HEAD)
    d = y - mean
    var = head_sum(d * d) * (1.0 / HEAD)
    yn = d * lax.rsqrt(var + LNX_EPS) * lnx_g + lnx_b
    bonus = head_sum(r * kf * r_k) * v
    o_ref[0] = ((yn + bonus) * gate).astype(BF16)


def _wkv(r, k, v, l1, w2, vec, n_chunks=2):
    B, S, C = r.shape
    T = n_chunks * CHUNK
    tok = lambda w: pl.BlockSpec((1, T, w), lambda b, s: (b, s, 0))
    return pl.pallas_call(
        functools.partial(_wkv_kernel, n_chunks=n_chunks),
        grid=(B, S // T),
        in_specs=[tok(C), tok(C), tok(C), tok(LORA_PAD),
                  pl.BlockSpec(w2.shape, lambda b, s: (0, 0, 0)),
                  pl.BlockSpec(vec.shape, lambda b, s: (0, 0))],
        out_specs=tok(C),
        out_shape=jax.ShapeDtypeStruct((B, S, C), BF16),
        scratch_shapes=[pltpu.VMEM((N_PAIRS, PAIR, PAIR), F32), pltpu.VMEM((T, C), F32)],
        compiler_params=pltpu.CompilerParams(dimension_semantics=("parallel", "arbitrary"),
                                             vmem_limit_bytes=VMEM_LIMIT),
        name="wkv",
    )(r, k, v, l1, w2, vec)


def _outproj_kernel(yd_ref, yr_ref, x_ref, ada_ref, wout_ref, g_ref, x1_ref, h2_ref):
    ada = ada_ref[0]
    gt_m, sh_f, sc_f = ada[2:3], ada[3:4], ada[4:5]
    y = _dot(yd_ref[0], wout_ref[0:DIFF_WIDTH, :]) + _dot(yr_ref[0], wout_ref[DIFF_WIDTH:, :])
    x1 = x_ref[0] + gt_m * (_rms(y, RMS_EPS) * g_ref[0:1, :])
    x1_ref[0] = x1
    h2 = _rms(x1, RMS_EPS) * g_ref[1:2, :]
    h2_ref[0] = (h2 * (1.0 + sc_f) + sh_f).astype(BF16)


def _outproj(y_diff, y_rwkv, x, ada3, w_out, g_post_pre, tm=512):
    B, S, D = x.shape
    tm = min(tm, S)
    tok = lambda w: pl.BlockSpec((1, tm, w), lambda b, s: (b, s, 0))
    return pl.pallas_call(
        _outproj_kernel,
        grid=(B, S // tm),
        in_specs=[tok(DIFF_WIDTH), tok(RWKV_WIDTH), tok(D),
                  pl.BlockSpec((1, 6, D), lambda b, s: (b, 0, 0)),
                  pl.BlockSpec(w_out.shape, lambda b, s: (0, 0)),
                  pl.BlockSpec(g_post_pre.shape, lambda b, s: (0, 0))],
        out_specs=[tok(D), tok(D)],
        out_shape=[jax.ShapeDtypeStruct((B, S, D), F32), jax.ShapeDtypeStruct((B, S, D), BF16)],
        compiler_params=pltpu.CompilerParams(dimension_semantics=("parallel", "parallel"),
                                             vmem_limit_bytes=VMEM_LIMIT),
        name="outproj",
    )(y_diff, y_rwkv, x, ada3, w_out, g_post_pre)


def _ffn_kernel(h_ref, x1_ref, ada_ref, wg_ref, wu_ref, wd_ref, g_ref, o_ref, acc_ref):
    j = pl.program_id(2)

    @pl.when(j == 0)
    def _():
        acc_ref[...] = jnp.zeros_like(acc_ref)

    h = h_ref[0]
    gate = _dot(h, wg_ref[...])
    up = _dot(h, wu_ref[...])
    act = gate * _sigmoid(gate) * up
    acc_ref[...] += _dot(act.astype(BF16), wd_ref[...])

    @pl.when(j == pl.num_programs(2) - 1)
    def _():
        gt_f = ada_ref[0][5:6]
        o_ref[0] = x1_ref[0] + gt_f * (_rms(acc_ref[...], RMS_EPS) * g_ref[...])


def _ffn(h2, x1, ada3, w_gate, w_up, w_down, g_post, tm=1024, tf=256):
    B, S, D = x1.shape
    tm = min(tm, S)
    tok = lambda: pl.BlockSpec((1, tm, D), lambda b, s, j: (b, s, 0))
    return pl.pallas_call(
        _ffn_kernel,
        grid=(B, S // tm, FFN_HIDDEN // tf),
        in_specs=[tok(), tok(),
                  pl.BlockSpec((1, 6, D), lambda b, s, j: (b, 0, 0)),
                  pl.BlockSpec((D, tf), lambda b, s, j: (0, j)),
                  pl.BlockSpec((D, tf), lambda b, s, j: (0, j)),
                  pl.BlockSpec((tf, D), lambda b, s, j: (j, 0)),
                  pl.BlockSpec(g_post.shape, lambda b, s, j: (0, 0))],
        out_specs=tok(),
        out_shape=jax.ShapeDtypeStruct((B, S, D), F32),
        scratch_shapes=[pltpu.VMEM((tm, D), F32)],
        compiler_params=pltpu.CompilerParams(
            dimension_semantics=("parallel", "parallel", "arbitrary"),
            vmem_limit_bytes=VMEM_LIMIT),
        name="ffn",
    )(h2, x1, ada3, w_gate, w_up, w_down, g_post)


def _pad_cols(w, start, width):
    return jnp.zeros((w.shape[0], width), w.dtype).at[:, start:start + w.shape[1]].set(w)


def _pad_rows(w, start, height):
    return jnp.zeros((height, w.shape[1]), w.dtype).at[start:start + w.shape[0], :].set(w)


def kernel(x, c, w_ada, b_ada, g_pre_mix, g_post_mix, g_pre_ffn, g_post_ffn, w_in, lam_q1, lam_k1, lam_q2, lam_k2, subln_g, mu_r, mu_k, mu_v, mu_w, mu_a, mu_g, w0, w1, w2, a0, a1, a2, g1, g2, k_k, k_a, r_k, lnx_g, lnx_b, w_out, w_gate, w_up, w_down):
    B, S, D = x.shape
    for l in range(w_in.shape[0]):
        lam_init = 0.8 - 0.6 * math.exp(-0.3 * l)
        ada3 = _ada(c, w_ada[l], b_ada[l]).reshape(B, 6, D)

        wl1 = jnp.stack([_pad_cols(w1[l], 0, LORA_PAD), _pad_cols(a1[l], 64, LORA_PAD),
                         _pad_cols(g1[l], 128, LORA_PAD)]).astype(BF16)
        wl2 = jnp.stack([_pad_rows(w2[l], 0, LORA_PAD), _pad_rows(a2[l], 64, LORA_PAD),
                         _pad_rows(g2[l], 128, LORA_PAD)]).astype(BF16)
        mu_h = jnp.stack([mu_w[l], mu_a[l], mu_g[l]])
        mu_rkv = jnp.stack([mu_r[l], mu_k[l], mu_v[l]])
        q, k, v, pr, pk, pv, l1 = _premix(x, ada3, g_pre_mix[l][None], w_in[l].astype(BF16),
                                          wl1, mu_h, mu_rkv)

        lamv = jnp.stack([lam_q1[l], lam_k1[l], lam_q2[l], lam_k2[l]])
        y_diff = _attn(q, k, v, lamv, subln_g[l][None], lam_init)

        vec = jnp.stack([w0[l], a0[l], k_k[l], k_a[l], r_k[l].reshape(-1), lnx_g[l], lnx_b[l],
                         jnp.zeros_like(w0[l])])
        y_rwkv = _wkv(pr, pk, pv, l1, wl2, vec)

        x1, h2 = _outproj(y_diff, y_rwkv, x, ada3, w_out[l].astype(BF16),
                          jnp.stack([g_post_mix[l], g_pre_ffn[l]]))
        x = _ffn(h2, x1, ada3, w_gate[l].astype(BF16), w_up[l].astype(BF16),
                 w_down[l].astype(BF16), g_post_ffn[l][None])
    return x
```
